```python
import jax, jax.numpy as jnp
from jax import lax
import numpy as np

D_MODEL = 1024
BATCH = 4
SEQ = 8192
DEPTH = 1

D_MIX = D_MODEL
D_GLA = D_MIX // 2
D_CONV = D_MIX - D_GLA
GLA_HEADS = 4
GLA_DV = D_GLA // GLA_HEADS
GLA_DK = GLA_DV // 2
GLA_DQK = GLA_HEADS * GLA_DK
GLA_LOWRANK = 16
GATE_NORMALIZER = 16.0
CHUNK = 64
CONF_KERNEL = 31
CONF_GROUPS = 4
FFN_HIDDEN = 2816
FFN_KERNEL = 3
N_MOD = 6
EPS = 1e-6
SPLITS = (GLA_DQK, 2 * GLA_DQK, 2 * GLA_DQK + D_GLA, 2 * GLA_DQK + 2 * D_GLA,
          2 * GLA_DQK + 2 * D_GLA + GLA_LOWRANK)
IN_COLS = 2 * GLA_DQK + 2 * D_GLA + GLA_LOWRANK + 2 * D_CONV

kernel_name = "hymba_gla_conformer_convffn_adaln"


def rms_norm(x, g):
    xf = x.astype(jnp.float32)
    y = xf * lax.rsqrt(jnp.mean(xf * xf, axis=-1, keepdims=True) + EPS)
    return (y * g.astype(jnp.float32)).astype(x.dtype)


def group_norm_tokenwise(x, g, b):
    B, S, C = x.shape
    xf = x.astype(jnp.float32).reshape(B, S, CONF_GROUPS, C // CONF_GROUPS)
    mu = jnp.mean(xf, axis=-1, keepdims=True)
    var = jnp.mean(jnp.square(xf - mu), axis=-1, keepdims=True)
    y = ((xf - mu) * lax.rsqrt(var + EPS)).reshape(B, S, C)
    return (y * g + b).astype(x.dtype)


def modulate(h, shift, scale):
    return h * (1.0 + scale[:, None, :]) + shift[:, None, :]


def causal_depthwise_conv(x, w, b):
    K, C = w.shape
    xp = jnp.pad(x, ((0, 0), (K - 1, 0), (0, 0)))
    y = lax.conv_general_dilated(xp, w[:, None, :].astype(x.dtype), window_strides=(1,),
                                 padding='VALID', dimension_numbers=('NWC', 'WIO', 'NWC'),
                                 feature_group_count=C)
    return y + b


def gla_chunked(q, k, v, log_a):
    B, S, H, DK = q.shape
    DV = v.shape[-1]
    N = S // CHUNK

    def to_chunks(t):
        return t.astype(jnp.float32).reshape(B, N, CHUNK, H, t.shape[-1]).transpose(0, 3, 1, 2, 4)

    qc = to_chunks(q) * (DK ** -0.5)
    kc, vc, la = to_chunks(k), to_chunks(v), to_chunks(log_a)
    G = jnp.cumsum(la, axis=3)
    G_ref = G[:, :, :, CHUNK // 2 - 1:CHUNK // 2]
    G_last = G[:, :, :, -1:]

    scores = jnp.einsum('bhnid,bhnjd->bhnij', qc * jnp.exp(G - G_ref), kc * jnp.exp(G_ref - G))
    causal = jnp.tril(jnp.ones((CHUNK, CHUNK), dtype=bool))
    scores = jnp.where(causal, scores, 0.0)
    o_intra = jnp.einsum('bhnij,bhnjv->bhniv', scores, vc)

    kv = jnp.einsum('bhncd,bhncv->nbhdv', kc * jnp.exp(G_last - G), vc)
    decay = jnp.exp(G_last[:, :, :, 0]).transpose(2, 0, 1, 3)

    def step(state, inp):
        dec, kv_n = inp
        return dec[..., None] * state + kv_n, state

    s0 = jnp.zeros((B, H, DK, DV), jnp.float32)
    _, s_prev = lax.scan(step, s0, (decay, kv))
    o_inter = jnp.einsum('bhncd,nbhdv->bhncv', qc * jnp.exp(G), s_prev)

    o = o_intra + o_inter
    return o.transpose(0, 2, 3, 1, 4).reshape(B, S, H, DV)


def setup_inputs(seed: int = 0) -> dict:
    key = jax.random.key(seed)
    ks = jax.random.split(key, 22)

    def nrm(k, shape, scale):
        return jax.random.normal(k, shape, jnp.float32) * scale

    F2 = 2 * FFN_HIDDEN
    return {
        "x": nrm(ks[0], (BATCH, SEQ, D_MODEL), 1.0),
        "c": nrm(ks[1], (BATCH, D_MODEL), 1.0),
        "ada_w": nrm(ks[2], (DEPTH, D_MODEL, N_MOD * D_MODEL), D_MODEL ** -0.5),
        "ada_b": nrm(ks[3], (DEPTH, N_MOD * D_MODEL), 0.02),
        "norm1_g": 1.0 + nrm(ks[4], (DEPTH, D_MODEL), 0.02),
        "w_in": nrm(ks[5], (DEPTH, D_MODEL, IN_COLS), D_MODEL ** -0.5),
        "w_alpha": nrm(ks[6], (DEPTH, GLA_LOWRANK, GLA_DQK), GLA_LOWRANK ** -0.5),
        "b_alpha": nrm(ks[7], (DEPTH, GLA_DQK), 0.1),
        "gla_norm_g": 1.0 + nrm(ks[8], (DEPTH, GLA_DV), 0.02),
        "conf_dw_w": nrm(ks[9], (DEPTH, CONF_KERNEL, D_CONV), CONF_KERNEL ** -0.5),
        "conf_dw_b": nrm(ks[10], (DEPTH, D_CONV), 0.02),
        "conf_gn_g": 1.0 + nrm(ks[11], (DEPTH, D_CONV), 0.02),
        "conf_gn_b": nrm(ks[12], (DEPTH, D_CONV), 0.02),
        "conf_pw_w": nrm(ks[13], (DEPTH, D_CONV, D_CONV), D_CONV ** -0.5),
        "conf_pw_b": nrm(ks[14], (DEPTH, D_CONV), 0.02),
        "w_out": nrm(ks[15], (DEPTH, D_MIX, D_MODEL), D_MIX ** -0.5),
        "norm2_g": 1.0 + nrm(ks[16], (DEPTH, D_MODEL), 0.02),
        "w_up": nrm(ks[17], (DEPTH, D_MODEL, F2), D_MODEL ** -0.5),
        "ffn_dw_w": nrm(ks[18], (DEPTH, FFN_KERNEL, F2), FFN_KERNEL ** -0.5),
        "ffn_dw_b": nrm(ks[19], (DEPTH, F2), 0.02),
        "w_down": nrm(ks[20], (DEPTH, FFN_HIDDEN, D_MODEL), FFN_HIDDEN ** -0.5),
        "final_g": 1.0 + nrm(ks[21], (D_MODEL,), 0.02),
    }


def reference(x, c, ada_w, ada_b, norm1_g, w_in, w_alpha, b_alpha, gla_norm_g,
              conf_dw_w, conf_dw_b, conf_gn_g, conf_gn_b, conf_pw_w, conf_pw_b, w_out,
              norm2_g, w_up, ffn_dw_w, ffn_dw_b, w_down, final_g):
    B, S, D = x.shape
    c_act = jax.nn.silu(c)
    for l in range(DEPTH):
        mod = c_act @ ada_w[l] + ada_b[l]
        sh1, sc1, gt1, sh2, sc2, gt2 = jnp.split(mod, N_MOD, axis=-1)

        h = modulate(rms_norm(x, norm1_g[l]), sh1, sc1)
        proj = h @ w_in[l]
        q, k, v, g_out, a_lr, u = jnp.split(proj, SPLITS, axis=-1)

        log_a = jax.nn.log_sigmoid((a_lr @ w_alpha[l] + b_alpha[l]).astype(jnp.float32)) / GATE_NORMALIZER
        o = gla_chunked(q.reshape(B, S, GLA_HEADS, GLA_DK),
                        k.reshape(B, S, GLA_HEADS, GLA_DK),
                        v.reshape(B, S, GLA_HEADS, GLA_DV),
                        log_a.reshape(B, S, GLA_HEADS, GLA_DK))
        o = rms_norm(o, gla_norm_g[l]).astype(x.dtype).reshape(B, S, D_GLA)
        o = o * jax.nn.silu(g_out)

        u_val, u_gate = jnp.split(u, 2, axis=-1)
        u = u_val * jax.nn.sigmoid(u_gate)
        u = causal_depthwise_conv(u, conf_dw_w[l], conf_dw_b[l])
        u = jax.nn.silu(group_norm_tokenwise(u, conf_gn_g[l], conf_gn_b[l]))
        u = u @ conf_pw_w[l] + conf_pw_b[l]

        y = jnp.concatenate([o, u], axis=-1) @ w_out[l]
        x = x + gt1[:, None, :] * y

        h = modulate(rms_norm(x, norm2_g[l]), sh2, sc2)
        z = causal_depthwise_conv(h @ w_up[l], ffn_dw_w[l], ffn_dw_b[l])
        z_gate, z_val = jnp.split(z, 2, axis=-1)
        x = x + gt2[:, None, :] * ((jax.nn.silu(z_gate) * z_val) @ w_down[l])

    return rms_norm(x, final_g)
```

```python
import functools

import jax
import jax.numpy as jnp
from jax import lax
from jax.experimental import pallas as pl
from jax.experimental.pallas import tpu as pltpu

F32 = jnp.float32
BF16 = jnp.bfloat16

GLA_HEADS = 4
GLA_LOWRANK = 16
GATE_NORMALIZER = 16.0
CHUNK = 64
CONF_KERNEL = 31
CONF_GROUPS = 4
FFN_KERNEL = 3
N_MOD = 6
EPS = 1e-6

SUBLANES = 8
CONF_HALO = 32
SEQ_TILE = 256
FFN_COL_CHUNK = 256
VMEM_LIMIT_BYTES = 56 * 1024 * 1024


def _dot(a, b):
    return jnp.dot(a, b, preferred_element_type=F32)


def _dot_nt(a, b):
    return lax.dot_general(a, b, (((1,), (1,)), ((), ())), preferred_element_type=F32)


def _dot_tn(a, b):
    return lax.dot_general(a, b, (((0,), (0,)), ((), ())), preferred_element_type=F32)


def _silu(x):
    return x * jax.nn.sigmoid(x)


def _rms(x, g):
    return x * lax.rsqrt(jnp.mean(x * x, axis=-1, keepdims=True) + EPS) * g


def _ada_kernel(c_ref, w_ref, b_ref, o_ref):
    c = c_ref[...]
    ca = _silu(c)
    c_hi = ca.astype(BF16)
    c_lo = (ca - c_hi.astype(F32)).astype(BF16)
    w = w_ref[...]
    w_hi = w.astype(BF16)
    w_lo = (w - w_hi.astype(F32)).astype(BF16)
    o_ref[...] = _dot(c_hi, w_hi) + _dot(c_lo, w_hi) + _dot(c_hi, w_lo) + b_ref[...]


def _ada_call(c, w, b):
    bsz, d = c.shape
    n = w.shape[1]
    tn = d
    return pl.pallas_call(
        _ada_kernel,
        grid=(n // tn,),
        in_specs=[pl.BlockSpec((bsz, d), lambda j: (0, 0)),
                  pl.BlockSpec((d, tn), lambda j: (0, j)),
                  pl.BlockSpec((1, tn), lambda j: (0, j))],
        out_specs=pl.BlockSpec((bsz, tn), lambda j: (0, j)),
        out_shape=jax.ShapeDtypeStruct((bsz, n), F32),
        name="adaln_mod",
    )(c, w, b.reshape(1, n))


def _mixer_kernel(x_ref, mod_ref, n1g_ref, wm_ref, wa_ref, wu_ref, walpha_ref, balpha_ref, glag_ref,
                  dww_ref, dwb_ref, gng_ref, gnb_ref, pww_ref, pwb_ref, wout_ref,
                  o_ref, st_ref, ubuf_ref, *, ts, dqk, dgla, dconv):
    dk = dqk // GLA_HEADS
    dv = dgla // GLA_HEADS
    gsz = dconv // CONF_GROUPS

    @pl.when(pl.program_id(1) == 0)
    def _():
        st_ref[...] = jnp.zeros_like(st_ref)
        ubuf_ref[0:CONF_HALO, :] = jnp.zeros((CONF_HALO, dconv), F32)

    x = x_ref[...]
    sh1 = mod_ref[0:1, :]
    sc1 = mod_ref[1:2, :]
    gt1 = mod_ref[2:3, :]
    hb = (_rms(x, n1g_ref[...]) * (1.0 + sc1) + sh1).astype(BF16)

    proj = _dot(hb, wm_ref[...])
    q = proj[:, 0:dqk] * (dk ** -0.5)
    k = proj[:, dqk:2 * dqk]
    v = proj[:, 2 * dqk:2 * dqk + dgla]
    g_out = proj[:, 2 * dqk + dgla:2 * dqk + 2 * dgla]

    a_lr = _dot(hb, wa_ref[...])
    z = _dot(a_lr.astype(BF16), walpha_ref[...]) + balpha_ref[...]
    log_a = (jnp.minimum(z, 0.0) - jnp.log1p(jnp.exp(-jnp.abs(z)))) * (1.0 / GATE_NORMALIZER)

    ri = lax.broadcasted_iota(jnp.int32, (ts, ts), 0)
    ci = lax.broadcasted_iota(jnp.int32, (ts, ts), 1)
    tri = jnp.where((ci <= ri) & (ri // CHUNK == ci // CHUNK), 1.0, 0.0).astype(BF16)
    la_hi = log_a.astype(BF16)
    la_lo = (log_a - la_hi.astype(F32)).astype(BF16)
    gcum = _dot(tri, la_hi) + _dot(tri, la_lo)

    lane = lax.broadcasted_iota(jnp.int32, (1, dqk), 1)
    head_masks = [((lane >= h * dk) & (lane < (h + 1) * dk)).astype(F32) for h in range(GLA_HEADS)]
    cr = lax.broadcasted_iota(jnp.int32, (CHUNK, CHUNK), 0)
    cc = lax.broadcasted_iota(jnp.int32, (CHUNK, CHUNK), 1)
    causal = cc <= cr
    glag = glag_ref[...]

    o_rows = []
    for c in range(ts // CHUNK):
        r0 = c * CHUNK
        gc = gcum[r0:r0 + CHUNK]
        g_ref = gc[CHUNK // 2 - 1:CHUNK // 2]
        g_last = gc[CHUNK - 1:CHUNK]
        qc = q[r0:r0 + CHUNK]
        kc = k[r0:r0 + CHUNK]
        qe = qc * jnp.exp(gc - g_ref)
        ke = (kc * jnp.exp(g_ref - gc)).astype(BF16)
        kd = kc * jnp.exp(g_last - gc)
        qg = qc * jnp.exp(gc)
        st = st_ref[...]
        st_b = st.astype(BF16)
        kv_new = jnp.zeros_like(st)
        o_heads = []
        for h in range(GLA_HEADS):
            m = head_masks[h]
            vh = v[r0:r0 + CHUNK, h * dv:(h + 1) * dv].astype(BF16)
            scores = _dot_nt((qe * m).astype(BF16), ke)
            scores = jnp.where(causal, scores, 0.0).astype(BF16)
            o_h = _dot(scores, vh) + _dot_nt((qg * m).astype(BF16), st_b)
            kv_new = kv_new + _dot_tn(vh, (kd * m).astype(BF16))
            o_h = _rms(o_h, glag) * _silu(g_out[r0:r0 + CHUNK, h * dv:(h + 1) * dv])
            o_heads.append(o_h)
        st_ref[...] = st * jnp.exp(g_last) + kv_new
        o_rows.append(jnp.concatenate(o_heads, axis=1))
    o_gla = jnp.concatenate(o_rows, axis=0).astype(BF16)

    u_val = _dot(hb, wu_ref[:, 0:dconv])
    u_gate = _dot(hb, wu_ref[:, dconv:2 * dconv])
    ubuf_ref[CONF_HALO:CONF_HALO + ts, :] = u_val * jax.nn.sigmoid(u_gate)
    ub = ubuf_ref[...]
    ubuf_ref[0:CONF_HALO, :] = ub[ts:ts + CONF_HALO]
    acc = jnp.zeros((ts, dconv), F32) + dwb_ref[...]
    for r in range(SUBLANES):
        rolled = ub if r == 0 else pltpu.roll(ub, r, axis=0)
        for a in range(CONF_HALO // SUBLANES):
            shift = SUBLANES * a + r
            if shift >= CONF_KERNEL:
                continue
            j = CONF_KERNEL - 1 - shift
            start = CONF_HALO - SUBLANES * a
            acc = acc + dww_ref[j:j + 1, :] * rolled[start:start + ts]
    groups = []
    for gi in range(CONF_GROUPS):
        seg = acc[:, gi * gsz:(gi + 1) * gsz]
        d = seg - jnp.mean(seg, axis=-1, keepdims=True)
        groups.append(d * lax.rsqrt(jnp.mean(d * d, axis=-1, keepdims=True) + EPS))
    un = jnp.concatenate(groups, axis=1) * gng_ref[...] + gnb_ref[...]
    u_out = (_dot(_silu(un).astype(BF16), pww_ref[...]) + pwb_ref[...]).astype(BF16)

    y = _dot(o_gla, wout_ref[0:dgla, :]) + _dot(u_out, wout_ref[dgla:dgla + dconv, :])
    o_ref[...] = x + gt1 * y


def _const_spec(shape):
    return pl.BlockSpec(shape, lambda b, s: (0,) * len(shape))


def _mixer_call(x, mod, n1g, wm, wa, wu, walpha, balpha, glag, dww, dwb, gng, gnb, pww, pwb, wout):
    bsz, seq, d = x.shape
    dqk = walpha.shape[1]
    dgla = (wm.shape[1] - 2 * dqk) // 2
    dconv = pww.shape[0]
    ts = min(SEQ_TILE, seq)
    consts = (n1g, wm, wa, wu, walpha, balpha, glag, dww, dwb, gng, gnb, pww, pwb, wout)
    kern = functools.partial(_mixer_kernel, ts=ts, dqk=dqk, dgla=dgla, dconv=dconv)
    return pl.pallas_call(
        kern,
        grid=(bsz, seq // ts),
        in_specs=[pl.BlockSpec((None, ts, d), lambda b, s: (b, s, 0)),
                  pl.BlockSpec((None, N_MOD, d), lambda b, s: (b, 0, 0))]
                 + [_const_spec(a.shape) for a in consts],
        out_specs=pl.BlockSpec((None, ts, d), lambda b, s: (b, s, 0)),
        out_shape=jax.ShapeDtypeStruct(x.shape, F32),
        scratch_shapes=[pltpu.VMEM((dgla // GLA_HEADS, dqk), F32),
                        pltpu.VMEM((CONF_HALO + ts, dconv), F32)],
        compiler_params=pltpu.CompilerParams(dimension_semantics=("arbitrary", "arbitrary"),
                                             vmem_limit_bytes=VMEM_LIMIT_BYTES),
        name="mixer",
    )(x, mod, *consts)


def _ffn_kernel(x_ref, mod_ref, n2g_ref, wup_ref, dww_ref, dwb_ref, wdown_ref, fg_ref,
                o_ref, carry_ref, *, ts, hidden, final_norm):
    @pl.when(pl.program_id(1) == 0)
    def _():
        carry_ref[...] = jnp.zeros_like(carry_ref)

    x = x_ref[...]
    sh2 = mod_ref[3:4, :]
    sc2 = mod_ref[4:5, :]
    gt2 = mod_ref[5:6, :]
    hb = (_rms(x, n2g_ref[...]) * (1.0 + sc2) + sh2).astype(BF16)

    def conv(cols):
        zc = _dot(hb, wup_ref[:, cols])
        zfull = jnp.concatenate([carry_ref[:, cols], zc], axis=0)
        carry_ref[:, cols] = zc[ts - SUBLANES:ts]
        out = dww_ref[FFN_KERNEL - 1:FFN_KERNEL, cols] * zc + dwb_ref[:, cols]
        for shift in range(1, FFN_KERNEL):
            prev = pltpu.roll(zfull, shift, axis=0)[SUBLANES:SUBLANES + ts]
            out = out + dww_ref[FFN_KERNEL - 1 - shift:FFN_KERNEL - shift, cols] * prev
        return out

    acc = jnp.zeros((ts, x.shape[1]), F32)
    for j in range(hidden // FFN_COL_CHUNK):
        lo = j * FFN_COL_CHUNK
        z_gate = conv(slice(lo, lo + FFN_COL_CHUNK))
        z_val = conv(slice(hidden + lo, hidden + lo + FFN_COL_CHUNK))
        act = (_silu(z_gate) * z_val).astype(BF16)
        acc = acc + _dot(act, wdown_ref[lo:lo + FFN_COL_CHUNK, :])
    x2 = x + gt2 * acc
    o_ref[...] = _rms(x2, fg_ref[...]) if final_norm else x2


def _ffn_call(x, mod, n2g, wup, dww, dwb, wdown, fg, final_norm):
    bsz, seq, d = x.shape
    hidden = wdown.shape[0]
    ts = min(SEQ_TILE, seq)
    consts = (n2g, wup, dww, dwb, wdown, fg)
    kern = functools.partial(_ffn_kernel, ts=ts, hidden=hidden, final_norm=final_norm)
    return pl.pallas_call(
        kern,
        grid=(bsz, seq // ts),
        in_specs=[pl.BlockSpec((None, ts, d), lambda b, s: (b, s, 0)),
                  pl.BlockSpec((None, N_MOD, d), lambda b, s: (b, 0, 0))]
                 + [_const_spec(a.shape) for a in consts],
        out_specs=pl.BlockSpec((None, ts, d), lambda b, s: (b, s, 0)),
        out_shape=jax.ShapeDtypeStruct(x.shape, F32),
        scratch_shapes=[pltpu.VMEM((SUBLANES, 2 * hidden), F32)],
        compiler_params=pltpu.CompilerParams(dimension_semantics=("arbitrary", "arbitrary"),
                                             vmem_limit_bytes=VMEM_LIMIT_BYTES),
        name="channel_mixer",
    )(x, mod, *consts)


def kernel(x, c, ada_w, ada_b, norm1_g, w_in, w_alpha, b_alpha, gla_norm_g, conf_dw_w, conf_dw_b,
           conf_gn_g, conf_gn_b, conf_pw_w, conf_pw_b, w_out, norm2_g, w_up, ffn_dw_w, ffn_dw_b,
           w_down, final_g):
    bsz, seq, d = x.shape
    depth = ada_w.shape[0]
    dqk = w_alpha.shape[2]
    dconv = conf_pw_w.shape[1]
    n_main = w_in.shape[2] - GLA_LOWRANK - 2 * dconv
    assert seq % CHUNK == 0 and seq % min(SEQ_TILE, seq) == 0
    row = lambda a: a.reshape(1, -1)
    for l in range(depth):
        mod = _ada_call(c, ada_w[l], ada_b[l]).reshape(bsz, N_MOD, d)
        x = _mixer_call(
            x, mod, row(norm1_g[l]),
            w_in[l][:, :n_main].astype(BF16),
            w_in[l][:, n_main:n_main + GLA_LOWRANK].astype(BF16),
            w_in[l][:, n_main + GLA_LOWRANK:].astype(BF16),
            w_alpha[l].astype(BF16), row(b_alpha[l]), row(gla_norm_g[l]),
            conf_dw_w[l], row(conf_dw_b[l]), row(conf_gn_g[l]), row(conf_gn_b[l]),
            conf_pw_w[l].astype(BF16), row(conf_pw_b[l]), w_out[l].astype(BF16))
        x = _ffn_call(x, mod, row(norm2_g[l]), w_up[l].astype(BF16), ffn_dw_w[l], row(ffn_dw_b[l]),
                      w_down[l].astype(BF16), row(final_g), final_norm=(l == depth - 1))
    return x
```

```python
import functools

import jax
import jax.numpy as jnp
from jax import lax
from jax.experimental import pallas as pl
from jax.experimental.pallas import tpu as pltpu

F32 = jnp.float32
BF16 = jnp.bfloat16

GLA_HEADS = 4
GLA_LOWRANK = 16
GATE_NORMALIZER = 16.0
CHUNK = 64
CONF_KERNEL = 31
CONF_GROUPS = 4
FFN_KERNEL = 3
N_MOD = 6
EPS = 1e-6

SUBLANES = 8
LANES = 128
CONV_ROW_BLOCK = 64
CONF_HALO = 32
SEQ_TILE = 256
FFN_COL_CHUNK = 256
FFN_LOOKAHEAD = 2
VMEM_LIMIT_BYTES = 56 * 1024 * 1024


def _dot(a, b):
    return jnp.dot(a, b, preferred_element_type=F32)


def _dot_nt(a, b):
    return lax.dot_general(a, b, (((1,), (1,)), ((), ())), preferred_element_type=F32)


def _dot_tn(a, b):
    return lax.dot_general(a, b, (((0,), (0,)), ((), ())), preferred_element_type=F32)


def _silu(x):
    return x * jax.nn.sigmoid(x)


def _rms(x, g):
    return x * lax.rsqrt(jnp.mean(x * x, axis=-1, keepdims=True) + EPS) * g


def _ada_kernel(c_ref, w_ref, b_ref, o_ref):
    c = c_ref[...]
    ca = _silu(c)
    c_hi = ca.astype(BF16)
    c_lo = (ca - c_hi.astype(F32)).astype(BF16)
    w = w_ref[...]
    w_hi = w.astype(BF16)
    w_lo = (w - w_hi.astype(F32)).astype(BF16)
    o_ref[...] = _dot(c_hi, w_hi) + _dot(c_lo, w_hi) + _dot(c_hi, w_lo) + b_ref[...]


def _ada_call(c, w, b):
    bsz, d = c.shape
    n = w.shape[1]
    tn = d
    return pl.pallas_call(
        _ada_kernel,
        grid=(n // tn,),
        in_specs=[pl.BlockSpec((bsz, d), lambda j: (0, 0)),
                  pl.BlockSpec((d, tn), lambda j: (0, j)),
                  pl.BlockSpec((1, tn), lambda j: (0, j))],
        out_specs=pl.BlockSpec((bsz, tn), lambda j: (0, j)),
        out_shape=jax.ShapeDtypeStruct((bsz, n), F32),
        name="adaln_mod",
    )(c, w, b.reshape(1, n))


def _mixer_kernel(x_ref, mod_ref, n1g_ref, wm_ref, wa_ref, wu_ref, walpha_ref, balpha_ref, glag_ref,
                  dww_ref, dwb_ref, gng_ref, gnb_ref, pww_ref, pwb_ref, wout_ref,
                  o_ref, st_ref, ubuf_ref, rbuf_ref, cbuf_ref, *, ts, dqk, dgla, dconv):
    dk = dqk // GLA_HEADS
    dv = dgla // GLA_HEADS
    gsz = dconv // CONF_GROUPS

    @pl.when(pl.program_id(1) == 0)
    def _():
        st_ref[...] = jnp.zeros_like(st_ref)
        ubuf_ref[0:CONF_HALO, :] = jnp.zeros((CONF_HALO, dconv), F32)

    x = x_ref[...]
    sh1 = mod_ref[0:1, :]
    sc1 = mod_ref[1:2, :]
    gt1 = mod_ref[2:3, :]
    hb = (_rms(x, n1g_ref[...]) * (1.0 + sc1) + sh1).astype(BF16)

    u_val = _dot(hb, wu_ref[:, 0:dconv])
    u_gate = _dot(hb, wu_ref[:, dconv:2 * dconv])
    ubuf_ref[CONF_HALO:CONF_HALO + ts, :] = u_val * jax.nn.sigmoid(u_gate)

    a_lr = _dot(hb, wa_ref[...])
    z = _dot(a_lr.astype(BF16), walpha_ref[...]) + balpha_ref[...]
    log_a = (jnp.minimum(z, 0.0) - jnp.log1p(jnp.exp(-jnp.abs(z)))) * (1.0 / GATE_NORMALIZER)

    ri = lax.broadcasted_iota(jnp.int32, (ts, ts), 0)
    ci = lax.broadcasted_iota(jnp.int32, (ts, ts), 1)
    tri = jnp.where((ci <= ri) & (ri // CHUNK == ci // CHUNK), 1.0, 0.0).astype(BF16)
    la_hi = log_a.astype(BF16)
    la_lo = (log_a - la_hi.astype(F32)).astype(BF16)
    gcum = _dot(tri, la_hi) + _dot(tri, la_lo)

    proj = _dot(hb, wm_ref[...])
    q = proj[:, 0:dqk] * (dk ** -0.5)
    k = proj[:, dqk:2 * dqk]
    v = proj[:, 2 * dqk:2 * dqk + dgla]
    g_out = proj[:, 2 * dqk + dgla:2 * dqk + 2 * dgla]

    ub = ubuf_ref[...]
    for r in range(1, SUBLANES):
        rbuf_ref[r - 1] = pltpu.roll(ub, r, axis=0)
    for lo in range(0, dconv, LANES):
        for rb in range(0, ts, CONV_ROW_BLOCK):
            acc = jnp.zeros((CONV_ROW_BLOCK, LANES), F32) + dwb_ref[:, lo:lo + LANES]
            for shift in range(CONF_KERNEL):
                a, r = divmod(shift, SUBLANES)
                start = CONF_HALO - SUBLANES * a + rb
                src_ref = ubuf_ref if r == 0 else rbuf_ref.at[r - 1]
                src = src_ref[start:start + CONV_ROW_BLOCK, lo:lo + LANES]
                j = CONF_KERNEL - 1 - shift
                acc = acc + dww_ref[j:j + 1, lo:lo + LANES] * src
            cbuf_ref[rb:rb + CONV_ROW_BLOCK, lo:lo + LANES] = acc
    ubuf_ref[0:CONF_HALO, :] = ubuf_ref[ts:ts + CONF_HALO, :]

    lane = lax.broadcasted_iota(jnp.int32, (1, dqk), 1)
    head_masks = [((lane >= h * dk) & (lane < (h + 1) * dk)).astype(F32) for h in range(GLA_HEADS)]
    cr = lax.broadcasted_iota(jnp.int32, (GLA_HEADS * CHUNK, CHUNK), 0)
    cc = lax.broadcasted_iota(jnp.int32, (GLA_HEADS * CHUNK, CHUNK), 1)
    causal = cc <= cr % CHUNK
    glag = glag_ref[...]

    o_rows = []
    for c in range(ts // CHUNK):
        r0 = c * CHUNK
        gc = gcum[r0:r0 + CHUNK]
        g_ref = gc[CHUNK // 2 - 1:CHUNK // 2]
        g_last = gc[CHUNK - 1:CHUNK]
        qc = q[r0:r0 + CHUNK]
        kc = k[r0:r0 + CHUNK]
        vc = v[r0:r0 + CHUNK].astype(BF16)
        qe = qc * jnp.exp(gc - g_ref)
        ke = (kc * jnp.exp(g_ref - gc)).astype(BF16)
        kd = (kc * jnp.exp(g_last - gc)).astype(BF16)
        qg = qc * jnp.exp(gc)
        qe_heads = jnp.concatenate([qe * m for m in head_masks], axis=0).astype(BF16)
        qg_heads = jnp.concatenate([qg * m for m in head_masks], axis=0).astype(BF16)
        st = st_ref[...]
        scores = _dot_nt(qe_heads, ke)
        kv_all = _dot_tn(vc, kd)
        o_inter = _dot_nt(qg_heads, st.astype(BF16))
        scores = jnp.where(causal, scores, 0.0).astype(BF16)
        kv_new = sum(kv_all[h * dv:(h + 1) * dv] * head_masks[h] for h in range(GLA_HEADS))
        st_ref[...] = st * jnp.exp(g_last) + kv_new
        o_heads = []
        for h in range(GLA_HEADS):
            o_h = _dot(scores[h * CHUNK:(h + 1) * CHUNK], vc[:, h * dv:(h + 1) * dv]) \
                + o_inter[h * CHUNK:(h + 1) * CHUNK]
            o_heads.append(_rms(o_h, glag) * _silu(g_out[r0:r0 + CHUNK, h * dv:(h + 1) * dv]))
        o_rows.append(jnp.concatenate(o_heads, axis=1))
    o_gla = jnp.concatenate(o_rows, axis=0).astype(BF16)
    y_gla = _dot(o_gla, wout_ref[0:dgla, :])

    conv = cbuf_ref[...]
    groups = []
    for gi in range(CONF_GROUPS):
        seg = conv[:, gi * gsz:(gi + 1) * gsz]
        d = seg - jnp.mean(seg, axis=-1, keepdims=True)
        groups.append(d * lax.rsqrt(jnp.mean(d * d, axis=-1, keepdims=True) + EPS))
    un = jnp.concatenate(groups, axis=1) * gng_ref[...] + gnb_ref[...]
    u_out = (_dot(_silu(un).astype(BF16), pww_ref[...]) + pwb_ref[...]).astype(BF16)

    y = y_gla + _dot(u_out, wout_ref[dgla:dgla + dconv, :])
    o_ref[...] = x + gt1 * y


def _const_spec(shape):
    return pl.BlockSpec(shape, lambda b, s: (0,) * len(shape))


def _mixer_call(x, mod, n1g, wm, wa, wu, walpha, balpha, glag, dww, dwb, gng, gnb, pww, pwb, wout):
    bsz, seq, d = x.shape
    dqk = walpha.shape[1]
    dgla = (wm.shape[1] - 2 * dqk) // 2
    dconv = pww.shape[0]
    ts = min(SEQ_TILE, seq)
    consts = (n1g, wm, wa, wu, walpha, balpha, glag, dww, dwb, gng, gnb, pww, pwb, wout)
    kern = functools.partial(_mixer_kernel, ts=ts, dqk=dqk, dgla=dgla, dconv=dconv)
    return pl.pallas_call(
        kern,
        grid=(bsz, seq // ts),
        in_specs=[pl.BlockSpec((None, ts, d), lambda b, s: (b, s, 0)),
                  pl.BlockSpec((None, N_MOD, d), lambda b, s: (b, 0, 0))]
                 + [_const_spec(a.shape) for a in consts],
        out_specs=pl.BlockSpec((None, ts, d), lambda b, s: (b, s, 0)),
        out_shape=jax.ShapeDtypeStruct(x.shape, F32),
        scratch_shapes=[pltpu.VMEM((dgla // GLA_HEADS, dqk), F32),
                        pltpu.VMEM((CONF_HALO + ts, dconv), F32),
                        pltpu.VMEM((SUBLANES - 1, CONF_HALO + ts, dconv), F32),
                        pltpu.VMEM((ts, dconv), F32)],
        compiler_params=pltpu.CompilerParams(dimension_semantics=("arbitrary", "arbitrary"),
                                             vmem_limit_bytes=VMEM_LIMIT_BYTES),
        name="mixer",
    )(x, mod, *consts)


def _ffn_kernel(x_ref, mod_ref, n2g_ref, wup_ref, dww_ref, dwb_ref, wdown_ref, fg_ref,
                o_ref, carry_ref, *, ts, hidden, final_norm):
    @pl.when(pl.program_id(1) == 0)
    def _():
        carry_ref[...] = jnp.zeros_like(carry_ref)

    x = x_ref[...]
    sh2 = mod_ref[3:4, :]
    sc2 = mod_ref[4:5, :]
    gt2 = mod_ref[5:6, :]
    hb = (_rms(x, n2g_ref[...]) * (1.0 + sc2) + sh2).astype(BF16)

    def up(j):
        lo = j * FFN_COL_CHUNK
        cols = (slice(lo, lo + FFN_COL_CHUNK), slice(hidden + lo, hidden + lo + FFN_COL_CHUNK))
        return [(c, _dot(hb, wup_ref[:, c])) for c in cols]

    def conv(cols, zc):
        zfull = jnp.concatenate([carry_ref[:, cols], zc], axis=0)
        carry_ref[:, cols] = zc[ts - SUBLANES:ts]
        out = dww_ref[FFN_KERNEL - 1:FFN_KERNEL, cols] * zc + dwb_ref[:, cols]
        for shift in range(1, FFN_KERNEL):
            prev = pltpu.roll(zfull, shift, axis=0)[SUBLANES:SUBLANES + ts]
            out = out + dww_ref[FFN_KERNEL - 1 - shift:FFN_KERNEL - shift, cols] * prev
        return out

    n_chunks = hidden // FFN_COL_CHUNK
    acc = jnp.zeros((ts, x.shape[1]), F32)
    z_ahead = [up(j) for j in range(FFN_LOOKAHEAD)]
    for j in range(n_chunks):
        if j + FFN_LOOKAHEAD < n_chunks:
            z_ahead.append(up(j + FFN_LOOKAHEAD))
        z_gate, z_val = [conv(c, zc) for c, zc in z_ahead.pop(0)]
        act = (_silu(z_gate) * z_val).astype(BF16)
        lo = j * FFN_COL_CHUNK
        acc = acc + _dot(act, wdown_ref[lo:lo + FFN_COL_CHUNK, :])
    x2 = x + gt2 * acc
    o_ref[...] = _rms(x2, fg_ref[...]) if final_norm else x2


def _ffn_call(x, mod, n2g, wup, dww, dwb, wdown, fg, final_norm):
    bsz, seq, d = x.shape
    hidden = wdown.shape[0]
    ts = min(SEQ_TILE, seq)
    consts = (n2g, wup, dww, dwb, wdown, fg)
    kern = functools.partial(_ffn_kernel, ts=ts, hidden=hidden, final_norm=final_norm)
    return pl.pallas_call(
        kern,
        grid=(bsz, seq // ts),
        in_specs=[pl.BlockSpec((None, ts, d), lambda b, s: (b, s, 0)),
                  pl.BlockSpec((None, N_MOD, d), lambda b, s: (b, 0, 0))]
                 + [_const_spec(a.shape) for a in consts],
        out_specs=pl.BlockSpec((None, ts, d), lambda b, s: (b, s, 0)),
        out_shape=jax.ShapeDtypeStruct(x.shape, F32),
        scratch_shapes=[pltpu.VMEM((SUBLANES, 2 * hidden), F32)],
        compiler_params=pltpu.CompilerParams(dimension_semantics=("arbitrary", "arbitrary"),
                                             vmem_limit_bytes=VMEM_LIMIT_BYTES),
        name="channel_mixer",
    )(x, mod, *consts)


def kernel(x, c, ada_w, ada_b, norm1_g, w_in, w_alpha, b_alpha, gla_norm_g, conf_dw_w, conf_dw_b,
           conf_gn_g, conf_gn_b, conf_pw_w, conf_pw_b, w_out, norm2_g, w_up, ffn_dw_w, ffn_dw_b,
           w_down, final_g):
    bsz, seq, d = x.shape
    depth = ada_w.shape[0]
    dconv = conf_pw_w.shape[1]
    n_main = w_in.shape[2] - GLA_LOWRANK - 2 * dconv
    assert seq % CHUNK == 0 and seq % min(SEQ_TILE, seq) == 0
    row = lambda a: a.reshape(1, -1)
    for l in range(depth):
        mod = _ada_call(c, ada_w[l], ada_b[l]).reshape(bsz, N_MOD, d)
        x = _mixer_call(
            x, mod, row(norm1_g[l]),
            w_in[l][:, :n_main].astype(BF16),
            w_in[l][:, n_main:n_main + GLA_LOWRANK].astype(BF16),
            w_in[l][:, n_main + GLA_LOWRANK:].astype(BF16),
            w_alpha[l].astype(BF16), row(b_alpha[l]), row(gla_norm_g[l]),
            conf_dw_w[l], row(conf_dw_b[l]), row(conf_gn_g[l]), row(conf_gn_b[l]),
            conf_pw_w[l].astype(BF16), row(conf_pw_b[l]), w_out[l].astype(BF16))
        x = _ffn_call(x, mod, row(norm2_g[l]), w_up[l].astype(BF16), ffn_dw_w[l], row(ffn_dw_b[l]),
                      w_down[l].astype(BF16), row(final_g), final_norm=(l == depth - 1))
    return x
```

```python
import functools

import jax
import jax.numpy as jnp
from jax import lax
from jax.experimental import pallas as pl
from jax.experimental.pallas import tpu as pltpu

F32 = jnp.float32
BF16 = jnp.bfloat16

GLA_HEADS = 4
GLA_LOWRANK = 16
GATE_NORMALIZER = 16.0
CHUNK = 64
CONF_KERNEL = 31
CONF_GROUPS = 4
FFN_KERNEL = 3
N_MOD = 6
EPS = 1e-6

SUBLANES = 8
LANES = 128
CONV_ROW_BLOCK = 64
CONF_HALO = 32
SEQ_TILE = 256
FFN_COL_CHUNK = 256
STAGE_ORDER = "FF M FF M FF M FF M FF M F M FF M FF M FF M FF M FF M F"
VMEM_LIMIT_BYTES = 56 * 1024 * 1024


def _dot(a, b):
    return jnp.dot(a, b, preferred_element_type=F32)


def _dot_nt(a, b):
    return lax.dot_general(a, b, (((1,), (1,)), ((), ())), preferred_element_type=F32)


def _dot_tn(a, b):
    return lax.dot_general(a, b, (((0,), (0,)), ((), ())), preferred_element_type=F32)


def _silu(x):
    return x * jax.nn.sigmoid(x)


def _rms(x, g):
    return x * lax.rsqrt(jnp.mean(x * x, axis=-1, keepdims=True) + EPS) * g


def _ada_kernel(c_ref, w_ref, b_ref, o_ref):
    c = c_ref[...]
    ca = _silu(c)
    c_hi = ca.astype(BF16)
    c_lo = (ca - c_hi.astype(F32)).astype(BF16)
    w = w_ref[...]
    w_hi = w.astype(BF16)
    w_lo = (w - w_hi.astype(F32)).astype(BF16)
    o_ref[...] = _dot(c_hi, w_hi) + _dot(c_lo, w_hi) + _dot(c_hi, w_lo) + b_ref[...]


def _ada_call(c, w, b):
    bsz, d = c.shape
    n = w.shape[1]
    tn = d
    return pl.pallas_call(
        _ada_kernel,
        grid=(n // tn,),
        in_specs=[pl.BlockSpec((bsz, d), lambda j: (0, 0)),
                  pl.BlockSpec((d, tn), lambda j: (0, j)),
                  pl.BlockSpec((1, tn), lambda j: (0, j))],
        out_specs=pl.BlockSpec((bsz, tn), lambda j: (0, j)),
        out_shape=jax.ShapeDtypeStruct((bsz, n), F32),
        name="adaln_mod",
    )(c, w, b.reshape(1, n))


def _mixer_stream(x_ref, mod_ref, n1g_ref, wm_ref, wa_ref, wu_ref, walpha_ref, balpha_ref, glag_ref,
                  dww_ref, dwb_ref, gng_ref, gnb_ref, pww_ref, pwb_ref, wout_ref,
                  n2g_ref, x1_out_ref, h2_out_ref, st_ref, ubuf_ref, rbuf_ref, cbuf_ref,
                  *, ts, dqk, dgla, dconv):
    dk = dqk // GLA_HEADS
    dv = dgla // GLA_HEADS
    gsz = dconv // CONF_GROUPS

    sh1 = mod_ref[0:1, :]
    sc1 = mod_ref[1:2, :]
    gt1 = mod_ref[2:3, :]
    hb = (_rms(x_ref[...], n1g_ref[...]) * (1.0 + sc1) + sh1).astype(BF16)

    def conformer_conv(c0, c1):
        u_val = _dot(hb, wu_ref[:, c0:c1])
        u_gate = _dot(hb, wu_ref[:, dconv + c0:dconv + c1])
        ubuf_ref[CONF_HALO:CONF_HALO + ts, c0:c1] = u_val * jax.nn.sigmoid(u_gate)
        ub = ubuf_ref[:, c0:c1]
        for r in range(1, SUBLANES):
            rbuf_ref[r - 1, :, c0:c1] = pltpu.roll(ub, r, axis=0)
        for lo in range(c0, c1, LANES):
            for rb in range(0, ts, CONV_ROW_BLOCK):
                acc = jnp.zeros((CONV_ROW_BLOCK, LANES), F32) + dwb_ref[:, lo:lo + LANES]
                for shift in range(CONF_KERNEL):
                    a, r = divmod(shift, SUBLANES)
                    start = CONF_HALO - SUBLANES * a + rb
                    src_ref = ubuf_ref if r == 0 else rbuf_ref.at[r - 1]
                    src = src_ref[start:start + CONV_ROW_BLOCK, lo:lo + LANES]
                    j = CONF_KERNEL - 1 - shift
                    acc = acc + dww_ref[j:j + 1, lo:lo + LANES] * src
                cbuf_ref[rb:rb + CONV_ROW_BLOCK, lo:lo + LANES] = acc
        ubuf_ref[0:CONF_HALO, c0:c1] = ubuf_ref[ts:ts + CONF_HALO, c0:c1]

    conformer_conv(0, dconv)
    yield

    a_lr = _dot(hb, wa_ref[...])
    z = _dot(a_lr.astype(BF16), walpha_ref[...]) + balpha_ref[...]
    log_a = (jnp.minimum(z, 0.0) - jnp.log1p(jnp.exp(-jnp.abs(z)))) * (1.0 / GATE_NORMALIZER)
    ri = lax.broadcasted_iota(jnp.int32, (ts, ts), 0)
    ci = lax.broadcasted_iota(jnp.int32, (ts, ts), 1)
    tri = jnp.where((ci <= ri) & (ri // CHUNK == ci // CHUNK), 1.0, 0.0).astype(BF16)
    la_hi = log_a.astype(BF16)
    la_lo = (log_a - la_hi.astype(F32)).astype(BF16)
    gcum = _dot(tri, la_hi) + _dot(tri, la_lo)
    yield

    proj = _dot(hb, wm_ref[...])
    q = proj[:, 0:dqk] * (dk ** -0.5)
    k = proj[:, dqk:2 * dqk]
    v = proj[:, 2 * dqk:2 * dqk + dgla]
    g_out = proj[:, 2 * dqk + dgla:2 * dqk + 2 * dgla]
    yield

    n_gla_chunks = ts // CHUNK
    lane = lax.broadcasted_iota(jnp.int32, (1, dqk), 1)
    head_masks = [((lane >= h * dk) & (lane < (h + 1) * dk)).astype(F32) for h in range(GLA_HEADS)]
    cr = lax.broadcasted_iota(jnp.int32, (GLA_HEADS * CHUNK, CHUNK), 0)
    cc = lax.broadcasted_iota(jnp.int32, (GLA_HEADS * CHUNK, CHUNK), 1)
    causal = cc <= cr % CHUNK
    glag = glag_ref[...]

    o_rows = []
    for c in range(n_gla_chunks):
        r0 = c * CHUNK
        gc = gcum[r0:r0 + CHUNK]
        g_ref = gc[CHUNK // 2 - 1:CHUNK // 2]
        g_last = gc[CHUNK - 1:CHUNK]
        qc = q[r0:r0 + CHUNK]
        kc = k[r0:r0 + CHUNK]
        vc = v[r0:r0 + CHUNK].astype(BF16)
        qe = qc * jnp.exp(gc - g_ref)
        ke = (kc * jnp.exp(g_ref - gc)).astype(BF16)
        kd = (kc * jnp.exp(g_last - gc)).astype(BF16)
        qg = qc * jnp.exp(gc)
        qe_heads = jnp.concatenate([qe * m for m in head_masks], axis=0).astype(BF16)
        qg_heads = jnp.concatenate([qg * m for m in head_masks], axis=0).astype(BF16)
        st = st_ref[...]
        scores = _dot_nt(qe_heads, ke)
        kv_all = _dot_tn(vc, kd)
        o_inter = _dot_nt(qg_heads, st.astype(BF16))
        scores = jnp.where(causal, scores, 0.0).astype(BF16)
        kv_new = sum(kv_all[h * dv:(h + 1) * dv] * head_masks[h] for h in range(GLA_HEADS))
        st_ref[...] = st * jnp.exp(g_last) + kv_new
        o_heads = []
        for h in range(GLA_HEADS):
            o_h = _dot(scores[h * CHUNK:(h + 1) * CHUNK], vc[:, h * dv:(h + 1) * dv]) \
                + o_inter[h * CHUNK:(h + 1) * CHUNK]
            o_heads.append(_rms(o_h, glag) * _silu(g_out[r0:r0 + CHUNK, h * dv:(h + 1) * dv]))
        o_rows.append(jnp.concatenate(o_heads, axis=1))
        yield
    o_gla = jnp.concatenate(o_rows, axis=0).astype(BF16)
    y_gla = _dot(o_gla, wout_ref[0:dgla, :])
    yield

    conv = cbuf_ref[...]
    groups = []
    for gi in range(CONF_GROUPS):
        seg = conv[:, gi * gsz:(gi + 1) * gsz]
        d = seg - jnp.mean(seg, axis=-1, keepdims=True)
        groups.append(d * lax.rsqrt(jnp.mean(d * d, axis=-1, keepdims=True) + EPS))
    un = jnp.concatenate(groups, axis=1) * gng_ref[...] + gnb_ref[...]
    u_out = (_dot(_silu(un).astype(BF16), pww_ref[...]) + pwb_ref[...]).astype(BF16)
    yield

    y = y_gla + _dot(u_out, wout_ref[dgla:dgla + dconv, :])
    x1 = x_ref[...] + gt1 * y
    x1_out_ref[...] = x1
    h2_out_ref[...] = (_rms(x1, n2g_ref[...]) * (1.0 + mod_ref[4:5, :]) + mod_ref[3:4, :]).astype(BF16)


def _ffn_stream(x1_ref, h2_ref, mod_ref, wup_ref, dww_ref, dwb_ref, wdown_ref, fg_ref,
                o_ref, carry_ref, *, ts, hidden, final_norm):
    gt2 = mod_ref[5:6, :]
    hb = h2_ref[...]

    def up(j):
        lo = j * FFN_COL_CHUNK
        cols = (slice(lo, lo + FFN_COL_CHUNK), slice(hidden + lo, hidden + lo + FFN_COL_CHUNK))
        return [(c, _dot(hb, wup_ref[:, c])) for c in cols]

    def conv(cols, zc):
        zfull = jnp.concatenate([carry_ref[:, cols], zc], axis=0)
        carry_ref[:, cols] = zc[ts - SUBLANES:ts]
        out = dww_ref[FFN_KERNEL - 1:FFN_KERNEL, cols] * zc + dwb_ref[:, cols]
        for shift in range(1, FFN_KERNEL):
            prev = pltpu.roll(zfull, shift, axis=0)[SUBLANES:SUBLANES + ts]
            out = out + dww_ref[FFN_KERNEL - 1 - shift:FFN_KERNEL - shift, cols] * prev
        return out

    n_chunks = hidden // FFN_COL_CHUNK
    zs = []
    for j in range(n_chunks):
        zs.append(up(j))
        yield
    acc = jnp.zeros((ts, x1_ref.shape[-1]), F32)
    for j in range(n_chunks):
        z_gate, z_val = [conv(c, zc) for c, zc in zs[j]]
        act = (_silu(z_gate) * z_val).astype(BF16)
        lo = j * FFN_COL_CHUNK
        acc = acc + _dot(act, wdown_ref[lo:lo + FFN_COL_CHUNK, :])
        if j + 1 < n_chunks:
            yield
    x2 = x1_ref[...] + gt2 * acc
    o_ref[...] = _rms(x2, fg_ref[...]) if final_norm else x2


N_MIXER_CONSTS = 15
N_FFN_CONSTS = 5


def _layer_kernel(x_ref, modm_ref, modf_ref, *refs, ts, n_seq, dqk, dgla, dconv, hidden, final_norm):
    mixer_consts = refs[:N_MIXER_CONSTS]
    ffn_consts = refs[N_MIXER_CONSTS:N_MIXER_CONSTS + N_FFN_CONSTS]
    o_ref, st_ref, ubuf_ref, rbuf_ref, cbuf_ref, carry_ref, x1_ref, h2_ref = refs[N_MIXER_CONSTS + N_FFN_CONSTS:]
    i = pl.program_id(0)

    @pl.when(i % n_seq == 0)
    def _():
        st_ref[...] = jnp.zeros_like(st_ref)
        ubuf_ref[0:CONF_HALO, :] = jnp.zeros((CONF_HALO, dconv), F32)

    @pl.when((i + n_seq - 1) % n_seq == 0)
    def _():
        carry_ref[...] = jnp.zeros_like(carry_ref)

    @pl.when(i == 0)
    def _():
        x1_ref[...] = jnp.zeros_like(x1_ref)
        h2_ref[...] = jnp.zeros_like(h2_ref)

    slot = i % 2
    mixer = _mixer_stream(x_ref, modm_ref, *mixer_consts, x1_ref.at[slot], h2_ref.at[slot], st_ref,
                          ubuf_ref, rbuf_ref, cbuf_ref, ts=ts, dqk=dqk, dgla=dgla, dconv=dconv)
    ffn = _ffn_stream(x1_ref.at[1 - slot], h2_ref.at[1 - slot], modf_ref, *ffn_consts, o_ref, carry_ref,
                      ts=ts, hidden=hidden, final_norm=final_norm)
    streams = {"F": ffn, "M": mixer}
    for who in STAGE_ORDER.replace(" ", ""):
        next(streams[who], None)
    for g in streams.values():
        for _ in g:
            pass


def _const_spec(shape):
    return pl.BlockSpec(shape, lambda i: (0,) * len(shape))


def _layer_call(x, mod, mixer_consts, ffn_consts, final_norm):
    bsz, seq, d = x.shape
    dqk = mixer_consts[4].shape[1]
    dconv = mixer_consts[11].shape[0]
    dgla = (mixer_consts[1].shape[1] - 2 * dqk) // 2
    hidden = ffn_consts[3].shape[0]
    ts = min(SEQ_TILE, seq)
    n_seq = seq // ts
    n_tiles = bsz * n_seq
    assert len(mixer_consts) == N_MIXER_CONSTS and len(ffn_consts) == N_FFN_CONSTS

    def mixer_tile(i):
        t = jnp.minimum(i, n_tiles - 1)
        return t // n_seq, t % n_seq

    def ffn_tile(i):
        t = jnp.maximum(i - 1, 0)
        return t // n_seq, t % n_seq

    kern = functools.partial(_layer_kernel, ts=ts, n_seq=n_seq, dqk=dqk, dgla=dgla, dconv=dconv,
                             hidden=hidden, final_norm=final_norm)
    return pl.pallas_call(
        kern,
        grid=(n_tiles + 1,),
        in_specs=[pl.BlockSpec((None, ts, d), lambda i: (*mixer_tile(i), 0)),
                  pl.BlockSpec((None, N_MOD, d), lambda i: (mixer_tile(i)[0], 0, 0)),
                  pl.BlockSpec((None, N_MOD, d), lambda i: (ffn_tile(i)[0], 0, 0))]
                 + [_const_spec(a.shape) for a in (*mixer_consts, *ffn_consts)],
        out_specs=pl.BlockSpec((None, ts, d), lambda i: (*ffn_tile(i), 0)),
        out_shape=jax.ShapeDtypeStruct(x.shape, F32),
        scratch_shapes=[pltpu.VMEM((dgla // GLA_HEADS, dqk), F32),
                        pltpu.VMEM((CONF_HALO + ts, dconv), F32),
                        pltpu.VMEM((SUBLANES - 1, CONF_HALO + ts, dconv), F32),
                        pltpu.VMEM((ts, dconv), F32),
                        pltpu.VMEM((SUBLANES, 2 * hidden), F32),
                        pltpu.VMEM((2, ts, d), F32),
                        pltpu.VMEM((2, ts, d), BF16)],
        compiler_params=pltpu.CompilerParams(dimension_semantics=("arbitrary",),
                                             vmem_limit_bytes=VMEM_LIMIT_BYTES),
        name="layer",
    )(x, mod, mod, *mixer_consts, *ffn_consts)


def kernel(x, c, ada_w, ada_b, norm1_g, w_in, w_alpha, b_alpha, gla_norm_g, conf_dw_w, conf_dw_b,
           conf_gn_g, conf_gn_b, conf_pw_w, conf_pw_b, w_out, norm2_g, w_up, ffn_dw_w, ffn_dw_b,
           w_down, final_g):
    bsz, seq, d = x.shape
    depth = ada_w.shape[0]
    dconv = conf_pw_w.shape[1]
    n_main = w_in.shape[2] - GLA_LOWRANK - 2 * dconv
    assert seq % CHUNK == 0 and seq % min(SEQ_TILE, seq) == 0
    row = lambda a: a.reshape(1, -1)
    for l in range(depth):
        mod = _ada_call(c, ada_w[l], ada_b[l]).reshape(bsz, N_MOD, d)
        mixer_consts = (
            row(norm1_g[l]),
            w_in[l][:, :n_main].astype(BF16),
            w_in[l][:, n_main:n_main + GLA_LOWRANK].astype(BF16),
            w_in[l][:, n_main + GLA_LOWRANK:].astype(BF16),
            w_alpha[l].astype(BF16), row(b_alpha[l]), row(gla_norm_g[l]),
            conf_dw_w[l], row(conf_dw_b[l]), row(conf_gn_g[l]), row(conf_gn_b[l]),
            conf_pw_w[l].astype(BF16), row(conf_pw_b[l]), w_out[l].astype(BF16), row(norm2_g[l]))
        ffn_consts = (w_up[l].astype(BF16), ffn_dw_w[l], row(ffn_dw_b[l]),
                      w_down[l].astype(BF16), row(final_g))
        x = _layer_call(x, mod, mixer_consts, ffn_consts, final_norm=(l == depth - 1))
    return x
```

```python
import functools

import jax
import jax.numpy as jnp
from jax import lax
from jax.experimental import pallas as pl
from jax.experimental.pallas import tpu as pltpu

F32 = jnp.float32
BF16 = jnp.bfloat16

GLA_HEADS = 4
GLA_LOWRANK = 16
GATE_NORMALIZER = 16.0
CHUNK = 64
CONF_KERNEL = 31
CONF_GROUPS = 4
FFN_KERNEL = 3
N_MOD = 6
EPS = 1e-6

SUBLANES = 8
LANES = 128
CONV_ROW_BLOCK = 64
CONF_HALO = 32
SEQ_TILE = 256
FFN_COL_CHUNK = 256
FFN_DOWN_CHUNKS = 4
ROW_STRIDE = 4
ROW_GROUP = ROW_STRIDE * SUBLANES
STAGE_ORDER = "FF M FF M FF M FF M FF M F M F M F M F M M"
VMEM_LIMIT_BYTES = 56 * 1024 * 1024


def _dot(a, b):
    return jnp.dot(a, b, preferred_element_type=F32)


def _dot_nt(a, b):
    return lax.dot_general(a, b, (((1,), (1,)), ((), ())), preferred_element_type=F32)


def _dot_tn(a, b):
    return lax.dot_general(a, b, (((0,), (0,)), ((), ())), preferred_element_type=F32)


def _silu(x):
    return x * jax.nn.sigmoid(x)


def _rms(x, g):
    return x * lax.rsqrt(jnp.mean(x * x, axis=-1, keepdims=True) + EPS) * g


def _ada_kernel(c_ref, w_ref, b_ref, o_ref):
    c = c_ref[...]
    ca = _silu(c)
    c_hi = ca.astype(BF16)
    c_lo = (ca - c_hi.astype(F32)).astype(BF16)
    w = w_ref[...]
    w_hi = w.astype(BF16)
    w_lo = (w - w_hi.astype(F32)).astype(BF16)
    o_ref[...] = _dot(c_hi, w_hi) + _dot(c_lo, w_hi) + _dot(c_hi, w_lo) + b_ref[...]


def _ada_call(c, w, b):
    bsz, d = c.shape
    n = w.shape[1]
    tn = d
    return pl.pallas_call(
        _ada_kernel,
        grid=(n // tn,),
        in_specs=[pl.BlockSpec((bsz, d), lambda j: (0, 0)),
                  pl.BlockSpec((d, tn), lambda j: (0, j)),
                  pl.BlockSpec((1, tn), lambda j: (0, j))],
        out_specs=pl.BlockSpec((bsz, tn), lambda j: (0, j)),
        out_shape=jax.ShapeDtypeStruct((bsz, n), F32),
        name="adaln_mod",
    )(c, w, b.reshape(1, n))


def _mixer_stream(x_ref, mod_ref, n1g_ref, wm_ref, wa_ref, wu_ref, walpha_ref, balpha_ref, glag_ref,
                  dww_ref, dwb_ref, gng_ref, gnb_ref, pww_ref, pwb_ref, wout_ref,
                  n2g_ref, x1_out_ref, h2_out_ref, st_ref, ubuf_ref, rbuf_ref, cbuf_ref,
                  *, ts, dqk, dgla, dconv):
    dk = dqk // GLA_HEADS
    dv = dgla // GLA_HEADS
    gsz = dconv // CONF_GROUPS

    sh1 = mod_ref[0:1, :]
    sc1 = mod_ref[1:2, :]
    gt1 = mod_ref[2:3, :]
    hb = (_rms(x_ref[...], n1g_ref[...]) * (1.0 + sc1) + sh1).astype(BF16)

    def conformer_conv(c0, c1):
        u_val = _dot(hb, wu_ref[:, c0:c1])
        u_gate = _dot(hb, wu_ref[:, dconv + c0:dconv + c1])
        ubuf_ref[CONF_HALO:CONF_HALO + ts, c0:c1] = u_val * jax.nn.sigmoid(u_gate)
        ub = ubuf_ref[:, c0:c1]
        for r in range(1, SUBLANES):
            rbuf_ref[r - 1, :, c0:c1] = pltpu.roll(ub, r, axis=0)
        for lo in range(c0, c1, LANES):
            for rb in range(0, ts, CONV_ROW_BLOCK):
                acc = jnp.zeros((CONV_ROW_BLOCK, LANES), F32) + dwb_ref[:, lo:lo + LANES]
                for shift in range(CONF_KERNEL):
                    a, r = divmod(shift, SUBLANES)
                    start = CONF_HALO - SUBLANES * a + rb
                    src_ref = ubuf_ref if r == 0 else rbuf_ref.at[r - 1]
                    src = src_ref[start:start + CONV_ROW_BLOCK, lo:lo + LANES]
                    j = CONF_KERNEL - 1 - shift
                    acc = acc + dww_ref[j:j + 1, lo:lo + LANES] * src
                cbuf_ref[rb:rb + CONV_ROW_BLOCK, lo:lo + LANES] = acc
        ubuf_ref[0:CONF_HALO, c0:c1] = ubuf_ref[ts:ts + CONF_HALO, c0:c1]

    conformer_conv(0, dconv)
    yield

    a_lr = _dot(hb, wa_ref[...])
    z = _dot(a_lr.astype(BF16), walpha_ref[...]) + balpha_ref[...]
    log_a = (jnp.minimum(z, 0.0) - jnp.log1p(jnp.exp(-jnp.abs(z)))) * (1.0 / GATE_NORMALIZER)
    ri = lax.broadcasted_iota(jnp.int32, (ts, ts), 0)
    ci = lax.broadcasted_iota(jnp.int32, (ts, ts), 1)
    tri = jnp.where((ci <= ri) & (ri // CHUNK == ci // CHUNK), 1.0, 0.0).astype(BF16)
    la_hi = log_a.astype(BF16)
    la_lo = (log_a - la_hi.astype(F32)).astype(BF16)
    gcum = _dot(tri, la_hi) + _dot(tri, la_lo)
    yield

    proj = _dot(hb, wm_ref[...])
    q = proj[:, 0:dqk] * (dk ** -0.5)
    k = proj[:, dqk:2 * dqk]
    v = proj[:, 2 * dqk:2 * dqk + dgla]
    g_out = proj[:, 2 * dqk + dgla:2 * dqk + 2 * dgla]
    yield

    n_gla_chunks = ts // CHUNK
    lane = lax.broadcasted_iota(jnp.int32, (1, dqk), 1)
    head_masks = [((lane >= h * dk) & (lane < (h + 1) * dk)).astype(F32) for h in range(GLA_HEADS)]
    cr = lax.broadcasted_iota(jnp.int32, (GLA_HEADS * CHUNK, CHUNK), 0)
    cc = lax.broadcasted_iota(jnp.int32, (GLA_HEADS * CHUNK, CHUNK), 1)
    causal = cc <= cr % CHUNK
    glag = glag_ref[...]

    o_rows = []
    for c in range(n_gla_chunks):
        r0 = c * CHUNK
        gc = gcum[r0:r0 + CHUNK]
        g_ref = gc[CHUNK // 2 - 1:CHUNK // 2]
        g_last = gc[CHUNK - 1:CHUNK]
        qc = q[r0:r0 + CHUNK]
        kc = k[r0:r0 + CHUNK]
        vc = v[r0:r0 + CHUNK].astype(BF16)
        qe = qc * jnp.exp(gc - g_ref)
        ke = (kc * jnp.exp(g_ref - gc)).astype(BF16)
        kd = (kc * jnp.exp(g_last - gc)).astype(BF16)
        qg = qc * jnp.exp(gc)
        qe_heads = jnp.concatenate([qe * m for m in head_masks], axis=0).astype(BF16)
        qg_heads = jnp.concatenate([qg * m for m in head_masks], axis=0).astype(BF16)
        st = st_ref[...]
        scores = _dot_nt(qe_heads, ke)
        kv_all = _dot_tn(vc, kd)
        o_inter = _dot_nt(qg_heads, st.astype(BF16))
        scores = jnp.where(causal, scores, 0.0).astype(BF16)
        kv_new = sum(kv_all[h * dv:(h + 1) * dv] * head_masks[h] for h in range(GLA_HEADS))
        st_ref[...] = st * jnp.exp(g_last) + kv_new
        o_heads = []
        for h in range(GLA_HEADS):
            o_h = _dot(scores[h * CHUNK:(h + 1) * CHUNK], vc[:, h * dv:(h + 1) * dv]) \
                + o_inter[h * CHUNK:(h + 1) * CHUNK]
            o_heads.append(_rms(o_h, glag) * _silu(g_out[r0:r0 + CHUNK, h * dv:(h + 1) * dv]))
        o_rows.append(jnp.concatenate(o_heads, axis=1))
        yield
    o_gla = jnp.concatenate(o_rows, axis=0).astype(BF16)
    y_gla = _dot(o_gla, wout_ref[0:dgla, :])
    yield

    conv = cbuf_ref[...]
    groups = []
    for gi in range(CONF_GROUPS):
        seg = conv[:, gi * gsz:(gi + 1) * gsz]
        d = seg - jnp.mean(seg, axis=-1, keepdims=True)
        groups.append(d * lax.rsqrt(jnp.mean(d * d, axis=-1, keepdims=True) + EPS))
    un = jnp.concatenate(groups, axis=1) * gng_ref[...] + gnb_ref[...]
    u_out = (_dot(_silu(un).astype(BF16), pww_ref[...]) + pwb_ref[...]).astype(BF16)
    yield

    y = y_gla + _dot(u_out, wout_ref[dgla:dgla + dconv, :])
    x1 = x_ref[...] + gt1 * y
    x1_out_ref[...] = x1
    h2 = _rms(x1, n2g_ref[...]) * (1.0 + mod_ref[4:5, :]) + mod_ref[3:4, :]
    for kk in range(h2.shape[1] // LANES):
        h2_out_ref[kk] = h2[:, kk * LANES:(kk + 1) * LANES]


def _ffn_stream(x1_ref, h2_ref, mod_ref, wup_ref, dww_ref, dwb_ref, wdown_ref, fg_ref,
                o_ref, carry_ref, unperm_ref, *, ts, hidden, final_norm):
    gt2 = mod_ref[5:6, :]
    d = x1_ref.shape[-1]
    n_groups = ts // ROW_GROUP
    pieces = [(g, i) for g in range(n_groups) for i in range(ROW_STRIDE)]

    def token_rows(g, i):
        return pl.ds(g * ROW_GROUP + i, SUBLANES, stride=ROW_STRIDE)

    hb = jnp.concatenate(
        [jnp.concatenate([h2_ref[kk, token_rows(g, i), :] for g, i in pieces], axis=0)
         for kk in range(d // LANES)], axis=1).astype(BF16)

    def up(j):
        lo = j * FFN_COL_CHUNK
        cols = (slice(lo, lo + FFN_COL_CHUNK), slice(hidden + lo, hidden + lo + FFN_COL_CHUNK))
        return [(c, _dot(hb, wup_ref[:, c])) for c in cols]

    def conv(cols, zc):
        blk = lambda g, i: zc[(g * ROW_STRIDE + i) * SUBLANES:(g * ROW_STRIDE + i + 1) * SUBLANES]

        def one_back(i):
            c0 = (i - (ROW_STRIDE - 2)) * SUBLANES
            full = jnp.concatenate([carry_ref[c0:c0 + SUBLANES, cols]] + [blk(g, i) for g in range(n_groups)], axis=0)
            return pltpu.roll(full, 1, axis=0)[SUBLANES:]

        back = {i: one_back(i) for i in (ROW_STRIDE - 2, ROW_STRIDE - 1)}
        for i in (ROW_STRIDE - 2, ROW_STRIDE - 1):
            c0 = (i - (ROW_STRIDE - 2)) * SUBLANES
            carry_ref[c0:c0 + SUBLANES, cols] = blk(n_groups - 1, i)

        def shifted(shift):
            out = []
            for g, i in pieces:
                if i >= shift:
                    out.append(blk(g, i - shift))
                else:
                    out.append(back[i - shift + ROW_STRIDE][g * SUBLANES:(g + 1) * SUBLANES])
            return jnp.concatenate(out, axis=0)

        out = dww_ref[FFN_KERNEL - 1:FFN_KERNEL, cols] * zc + dwb_ref[:, cols]
        for shift in range(1, FFN_KERNEL):
            out = out + dww_ref[FFN_KERNEL - 1 - shift:FFN_KERNEL - shift, cols] * shifted(shift)
        return out

    n_chunks = hidden // FFN_COL_CHUNK
    zs = []
    for j in range(n_chunks):
        zs.append(up(j))
        yield
    acc = None
    for j0 in range(0, n_chunks, FFN_DOWN_CHUNKS):
        acts = []
        for j in range(j0, min(j0 + FFN_DOWN_CHUNKS, n_chunks)):
            z_gate, z_val = [conv(c, zc) for c, zc in zs[j]]
            acts.append((_silu(z_gate) * z_val).astype(BF16))
        lo, hi = j0 * FFN_COL_CHUNK, (j0 + len(acts)) * FFN_COL_CHUNK
        part = _dot(jnp.concatenate(acts, axis=1), wdown_ref[lo:hi, :])
        acc = part if acc is None else acc + part
        if hi < hidden:
            yield
    for kk in range(d // LANES):
        for g, i in pieces:
            r0 = (g * ROW_STRIDE + i) * SUBLANES
            unperm_ref[kk, token_rows(g, i), :] = acc[r0:r0 + SUBLANES, kk * LANES:(kk + 1) * LANES]
    acc = jnp.concatenate([unperm_ref[kk] for kk in range(d // LANES)], axis=1)
    x2 = x1_ref[...] + gt2 * acc
    o_ref[...] = _rms(x2, fg_ref[...]) if final_norm else x2


N_MIXER_CONSTS = 15
N_FFN_CONSTS = 5


def _layer_kernel(x_ref, modm_ref, modf_ref, *refs, ts, n_seq, dqk, dgla, dconv, hidden, final_norm):
    mixer_consts = refs[:N_MIXER_CONSTS]
    ffn_consts = refs[N_MIXER_CONSTS:N_MIXER_CONSTS + N_FFN_CONSTS]
    (o_ref, st_ref, ubuf_ref, rbuf_ref, cbuf_ref, carry_ref, x1_ref, h2_ref,
     unperm_ref) = refs[N_MIXER_CONSTS + N_FFN_CONSTS:]
    i = pl.program_id(0)

    @pl.when(i % n_seq == 0)
    def _():
        st_ref[...] = jnp.zeros_like(st_ref)
        ubuf_ref[0:CONF_HALO, :] = jnp.zeros((CONF_HALO, dconv), F32)

    @pl.when((i + n_seq - 1) % n_seq == 0)
    def _():
        carry_ref[...] = jnp.zeros_like(carry_ref)

    @pl.when(i == 0)
    def _():
        x1_ref[...] = jnp.zeros_like(x1_ref)
        h2_ref[...] = jnp.zeros_like(h2_ref)

    slot = i % 2
    mixer = _mixer_stream(x_ref, modm_ref, *mixer_consts, x1_ref.at[slot], h2_ref.at[slot], st_ref,
                          ubuf_ref, rbuf_ref, cbuf_ref, ts=ts, dqk=dqk, dgla=dgla, dconv=dconv)
    ffn = _ffn_stream(x1_ref.at[1 - slot], h2_ref.at[1 - slot], modf_ref, *ffn_consts, o_ref, carry_ref,
                      unperm_ref, ts=ts, hidden=hidden, final_norm=final_norm)
    streams = {"F": ffn, "M": mixer}
    for who in STAGE_ORDER.replace(" ", ""):
        next(streams[who], None)
    for g in streams.values():
        for _ in g:
            pass


def _const_spec(shape):
    return pl.BlockSpec(shape, lambda i: (0,) * len(shape))


def _layer_call(x, mod, mixer_consts, ffn_consts, final_norm):
    bsz, seq, d = x.shape
    dqk = mixer_consts[4].shape[1]
    dconv = mixer_consts[11].shape[0]
    dgla = (mixer_consts[1].shape[1] - 2 * dqk) // 2
    hidden = ffn_consts[3].shape[0]
    ts = min(SEQ_TILE, seq)
    n_seq = seq // ts
    n_tiles = bsz * n_seq
    assert len(mixer_consts) == N_MIXER_CONSTS and len(ffn_consts) == N_FFN_CONSTS

    def mixer_tile(i):
        t = jnp.minimum(i, n_tiles - 1)
        return t // n_seq, t % n_seq

    def ffn_tile(i):
        t = jnp.maximum(i - 1, 0)
        return t // n_seq, t % n_seq

    kern = functools.partial(_layer_kernel, ts=ts, n_seq=n_seq, dqk=dqk, dgla=dgla, dconv=dconv,
                             hidden=hidden, final_norm=final_norm)
    return pl.pallas_call(
        kern,
        grid=(n_tiles + 1,),
        in_specs=[pl.BlockSpec((None, ts, d), lambda i: (*mixer_tile(i), 0)),
                  pl.BlockSpec((None, N_MOD, d), lambda i: (mixer_tile(i)[0], 0, 0)),
                  pl.BlockSpec((None, N_MOD, d), lambda i: (ffn_tile(i)[0], 0, 0))]
                 + [_const_spec(a.shape) for a in (*mixer_consts, *ffn_consts)],
        out_specs=pl.BlockSpec((None, ts, d), lambda i: (*ffn_tile(i), 0)),
        out_shape=jax.ShapeDtypeStruct(x.shape, F32),
        scratch_shapes=[pltpu.VMEM((dgla // GLA_HEADS, dqk), F32),
                        pltpu.VMEM((CONF_HALO + ts, dconv), F32),
                        pltpu.VMEM((SUBLANES - 1, CONF_HALO + ts, dconv), F32),
                        pltpu.VMEM((ts, dconv), F32),
                        pltpu.VMEM((2 * SUBLANES, 2 * hidden), F32),
                        pltpu.VMEM((2, ts, d), F32),
                        pltpu.VMEM((2, d // LANES, ts, LANES), F32),
                        pltpu.VMEM((d // LANES, ts, LANES), F32)],
        compiler_params=pltpu.CompilerParams(dimension_semantics=("arbitrary",),
                                             vmem_limit_bytes=VMEM_LIMIT_BYTES),
        name="layer",
    )(x, mod, mod, *mixer_consts, *ffn_consts)


def kernel(x, c, ada_w, ada_b, norm1_g, w_in, w_alpha, b_alpha, gla_norm_g, conf_dw_w, conf_dw_b,
           conf_gn_g, conf_gn_b, conf_pw_w, conf_pw_b, w_out, norm2_g, w_up, ffn_dw_w, ffn_dw_b,
           w_down, final_g):
    bsz, seq, d = x.shape
    depth = ada_w.shape[0]
    dconv = conf_pw_w.shape[1]
    n_main = w_in.shape[2] - GLA_LOWRANK - 2 * dconv
    assert seq % CHUNK == 0 and seq % min(SEQ_TILE, seq) == 0 and min(SEQ_TILE, seq) % ROW_GROUP == 0
    row = lambda a: a.reshape(1, -1)
    for l in range(depth):
        mod = _ada_call(c, ada_w[l], ada_b[l]).reshape(bsz, N_MOD, d)
        mixer_consts = (
            row(norm1_g[l]),
            w_in[l][:, :n_main].astype(BF16),
            w_in[l][:, n_main:n_main + GLA_LOWRANK].astype(BF16),
            w_in[l][:, n_main + GLA_LOWRANK:].astype(BF16),
            w_alpha[l].astype(BF16), row(b_alpha[l]), row(gla_norm_g[l]),
            conf_dw_w[l], row(conf_dw_b[l]), row(conf_gn_g[l]), row(conf_gn_b[l]),
            conf_pw_w[l].astype(BF16), row(conf_pw_b[l]), w_out[l].astype(BF16), row(norm2_g[l]))
        ffn_consts = (w_up[l].astype(BF16), ffn_dw_w[l], row(ffn_dw_b[l]),
                      w_down[l].astype(BF16), row(final_g))
        x = _layer_call(x, mod, mixer_consts, ffn_consts, final_norm=(l == depth - 1))
    return x
```

```python
import functools

import jax
import jax.numpy as jnp
from jax import lax
from jax.experimental import pallas as pl
from jax.experimental.pallas import tpu as pltpu

F32 = jnp.float32
BF16 = jnp.bfloat16

GLA_HEADS = 4
GLA_LOWRANK = 16
GATE_NORMALIZER = 16.0
CHUNK = 64
CONF_KERNEL = 31
CONF_GROUPS = 4
FFN_KERNEL = 3
N_MOD = 6
EPS = 1e-6

SUBLANES = 8
LANES = 128
CONV_ROW_BLOCK = 64
CONF_HALO = 32
SEQ_TILE = 256
FFN_COL_CHUNK = 256
FFN_DOWN_CHUNKS = 4
ROW_STRIDE = 4
ROW_GROUP = ROW_STRIDE * SUBLANES
STAGE_ORDER = "FF M FF M F M F M FF M FF M F M M F M F M M M F M"
VMEM_LIMIT_BYTES = 56 * 1024 * 1024


def _dot(a, b):
    return jnp.dot(a, b, preferred_element_type=F32)


def _dot_nt(a, b):
    return lax.dot_general(a, b, (((1,), (1,)), ((), ())), preferred_element_type=F32)


def _dot_tn(a, b):
    return lax.dot_general(a, b, (((0,), (0,)), ((), ())), preferred_element_type=F32)


def _silu(x):
    return x * jax.nn.sigmoid(x)


def _rms(x, g):
    return x * lax.rsqrt(jnp.mean(x * x, axis=-1, keepdims=True) + EPS) * g


def _ada_kernel(c_ref, w_ref, b_ref, o_ref):
    c = c_ref[...]
    ca = _silu(c)
    c_hi = ca.astype(BF16)
    c_lo = (ca - c_hi.astype(F32)).astype(BF16)
    w = w_ref[...]
    w_hi = w.astype(BF16)
    w_lo = (w - w_hi.astype(F32)).astype(BF16)
    o_ref[...] = _dot(c_hi, w_hi) + _dot(c_lo, w_hi) + _dot(c_hi, w_lo) + b_ref[...]


def _ada_call(c, w, b):
    bsz, d = c.shape
    n = w.shape[1]
    tn = d
    return pl.pallas_call(
        _ada_kernel,
        grid=(n // tn,),
        in_specs=[pl.BlockSpec((bsz, d), lambda j: (0, 0)),
                  pl.BlockSpec((d, tn), lambda j: (0, j)),
                  pl.BlockSpec((1, tn), lambda j: (0, j))],
        out_specs=pl.BlockSpec((bsz, tn), lambda j: (0, j)),
        out_shape=jax.ShapeDtypeStruct((bsz, n), F32),
        name="adaln_mod",
    )(c, w, b.reshape(1, n))


def _mixer_stream(x_ref, mod_ref, n1g_ref, wm_ref, wa_ref, wu_ref, walpha_ref, balpha_ref, glag_ref,
                  dww_ref, dwb_ref, gng_ref, gnb_ref, pww_ref, pwb_ref, wout_ref,
                  n2g_ref, x1_out_ref, h2_out_ref, st_ref, ubuf_ref, rbuf_ref, cbuf_ref,
                  *, ts, dqk, dgla, dconv):
    dk = dqk // GLA_HEADS
    dv = dgla // GLA_HEADS
    gsz = dconv // CONF_GROUPS

    sh1 = mod_ref[0:1, :]
    sc1 = mod_ref[1:2, :]
    gt1 = mod_ref[2:3, :]
    hb = (_rms(x_ref[...], n1g_ref[...]) * (1.0 + sc1) + sh1).astype(BF16)

    def conformer_conv(c0, c1):
        u_val = _dot(hb, wu_ref[:, c0:c1])
        u_gate = _dot(hb, wu_ref[:, dconv + c0:dconv + c1])
        ubuf_ref[CONF_HALO:CONF_HALO + ts, c0:c1] = u_val * jax.nn.sigmoid(u_gate)
        ub = ubuf_ref[:, c0:c1]
        for r in range(1, SUBLANES):
            rbuf_ref[r - 1, :, c0:c1] = pltpu.roll(ub, r, axis=0)
        for lo in range(c0, c1, LANES):
            for rb in range(0, ts, CONV_ROW_BLOCK):
                acc = jnp.zeros((CONV_ROW_BLOCK, LANES), F32) + dwb_ref[:, lo:lo + LANES]
                for shift in range(CONF_KERNEL):
                    a, r = divmod(shift, SUBLANES)
                    start = CONF_HALO - SUBLANES * a + rb
                    src_ref = ubuf_ref if r == 0 else rbuf_ref.at[r - 1]
                    src = src_ref[start:start + CONV_ROW_BLOCK, lo:lo + LANES]
                    j = CONF_KERNEL - 1 - shift
                    acc = acc + dww_ref[j:j + 1, lo:lo + LANES] * src
                cbuf_ref[rb:rb + CONV_ROW_BLOCK, lo:lo + LANES] = acc
        ubuf_ref[0:CONF_HALO, c0:c1] = ubuf_ref[ts:ts + CONF_HALO, c0:c1]

    conformer_conv(0, dconv)
    yield

    a_lr = _dot(hb, wa_ref[...])
    yield
    z = _dot(a_lr.astype(BF16), walpha_ref[...]) + balpha_ref[...]
    log_a = (jnp.minimum(z, 0.0) - jnp.log1p(jnp.exp(-jnp.abs(z)))) * (1.0 / GATE_NORMALIZER)
    yield
    ri = lax.broadcasted_iota(jnp.int32, (ts, ts), 0)
    ci = lax.broadcasted_iota(jnp.int32, (ts, ts), 1)
    tri = jnp.where((ci <= ri) & (ri // CHUNK == ci // CHUNK), 1.0, 0.0).astype(BF16)
    la_hi = log_a.astype(BF16)
    la_lo = (log_a - la_hi.astype(F32)).astype(BF16)
    gcum = _dot(tri, la_hi) + _dot(tri, la_lo)
    yield

    proj = _dot(hb, wm_ref[...])
    q = proj[:, 0:dqk] * (dk ** -0.5)
    k = proj[:, dqk:2 * dqk]
    v = proj[:, 2 * dqk:2 * dqk + dgla]
    g_out = proj[:, 2 * dqk + dgla:2 * dqk + 2 * dgla]
    yield

    n_gla_chunks = ts // CHUNK
    lane = lax.broadcasted_iota(jnp.int32, (1, dqk), 1)
    head_masks = [((lane >= h * dk) & (lane < (h + 1) * dk)).astype(F32) for h in range(GLA_HEADS)]
    cr = lax.broadcasted_iota(jnp.int32, (GLA_HEADS * CHUNK, CHUNK), 0)
    cc = lax.broadcasted_iota(jnp.int32, (GLA_HEADS * CHUNK, CHUNK), 1)
    causal = cc <= cr % CHUNK
    glag = glag_ref[...]

    def gla_front(c):
        r0 = c * CHUNK
        gc = gcum[r0:r0 + CHUNK]
        g_ref = gc[CHUNK // 2 - 1:CHUNK // 2]
        g_last = gc[CHUNK - 1:CHUNK]
        qc = q[r0:r0 + CHUNK]
        kc = k[r0:r0 + CHUNK]
        vc = v[r0:r0 + CHUNK].astype(BF16)
        qe = qc * jnp.exp(gc - g_ref)
        ke = (kc * jnp.exp(g_ref - gc)).astype(BF16)
        kd = (kc * jnp.exp(g_last - gc)).astype(BF16)
        qg = qc * jnp.exp(gc)
        qe_heads = jnp.concatenate([qe * m for m in head_masks], axis=0).astype(BF16)
        qg_heads = jnp.concatenate([qg * m for m in head_masks], axis=0).astype(BF16)
        scores = _dot_nt(qe_heads, ke)
        kv_all = _dot_tn(vc, kd)
        return r0, g_last, vc, qg_heads, scores, kv_all

    def gla_back(r0, g_last, vc, qg_heads, scores, kv_all):
        st = st_ref[...]
        o_inter = _dot_nt(qg_heads, st.astype(BF16))
        scores = jnp.where(causal, scores, 0.0).astype(BF16)
        kv_new = sum(kv_all[h * dv:(h + 1) * dv] * head_masks[h] for h in range(GLA_HEADS))
        st_ref[...] = st * jnp.exp(g_last) + kv_new
        o_heads = []
        for h in range(GLA_HEADS):
            o_h = _dot(scores[h * CHUNK:(h + 1) * CHUNK], vc[:, h * dv:(h + 1) * dv]) \
                + o_inter[h * CHUNK:(h + 1) * CHUNK]
            o_heads.append(_rms(o_h, glag) * _silu(g_out[r0:r0 + CHUNK, h * dv:(h + 1) * dv]))
        return jnp.concatenate(o_heads, axis=1)

    o_rows = []
    front = gla_front(0)
    yield
    for c in range(n_gla_chunks):
        nxt = gla_front(c + 1) if c + 1 < n_gla_chunks else None
        o_rows.append(gla_back(*front))
        front = nxt
        yield
    o_gla = jnp.concatenate(o_rows, axis=0).astype(BF16)
    y_gla = _dot(o_gla, wout_ref[0:dgla, :])
    yield

    conv = cbuf_ref[...]
    groups = []
    for gi in range(CONF_GROUPS):
        seg = conv[:, gi * gsz:(gi + 1) * gsz]
        d = seg - jnp.mean(seg, axis=-1, keepdims=True)
        groups.append(d * lax.rsqrt(jnp.mean(d * d, axis=-1, keepdims=True) + EPS))
    un = jnp.concatenate(groups, axis=1) * gng_ref[...] + gnb_ref[...]
    u_out = (_dot(_silu(un).astype(BF16), pww_ref[...]) + pwb_ref[...]).astype(BF16)
    yield

    y = y_gla + _dot(u_out, wout_ref[dgla:dgla + dconv, :])
    x1 = x_ref[...] + gt1 * y
    x1_out_ref[...] = x1
    h2 = _rms(x1, n2g_ref[...]) * (1.0 + mod_ref[4:5, :]) + mod_ref[3:4, :]
    for kk in range(h2.shape[1] // LANES):
        h2_out_ref[kk] = h2[:, kk * LANES:(kk + 1) * LANES]


def _ffn_stream(x1_ref, h2_ref, mod_ref, wup_ref, dww_ref, dwb_ref, wdown_ref, fg_ref,
                o_ref, carry_ref, unperm_ref, *, ts, hidden, final_norm):
    gt2 = mod_ref[5:6, :]
    d = x1_ref.shape[-1]
    n_groups = ts // ROW_GROUP
    pieces = [(g, i) for g in range(n_groups) for i in range(ROW_STRIDE)]

    def token_rows(g, i):
        return pl.ds(g * ROW_GROUP + i, SUBLANES, stride=ROW_STRIDE)

    hb = jnp.concatenate(
        [jnp.concatenate([h2_ref[kk, token_rows(g, i), :] for g, i in pieces], axis=0)
         for kk in range(d // LANES)], axis=1).astype(BF16)

    def up(j):
        lo = j * FFN_COL_CHUNK
        cols = (slice(lo, lo + FFN_COL_CHUNK), slice(hidden + lo, hidden + lo + FFN_COL_CHUNK))
        return [(c, _dot(hb, wup_ref[:, c])) for c in cols]

    def conv(cols, zc):
        blk = lambda g, i: zc[(g * ROW_STRIDE + i) * SUBLANES:(g * ROW_STRIDE + i + 1) * SUBLANES]

        def one_back(i):
            c0 = (i - (ROW_STRIDE - 2)) * SUBLANES
            full = jnp.concatenate([carry_ref[c0:c0 + SUBLANES, cols]] + [blk(g, i) for g in range(n_groups)], axis=0)
            return pltpu.roll(full, 1, axis=0)[SUBLANES:]

        back = {i: one_back(i) for i in (ROW_STRIDE - 2, ROW_STRIDE - 1)}
        for i in (ROW_STRIDE - 2, ROW_STRIDE - 1):
            c0 = (i - (ROW_STRIDE - 2)) * SUBLANES
            carry_ref[c0:c0 + SUBLANES, cols] = blk(n_groups - 1, i)

        def shifted(shift):
            out = []
            for g, i in pieces:
                if i >= shift:
                    out.append(blk(g, i - shift))
                else:
                    out.append(back[i - shift + ROW_STRIDE][g * SUBLANES:(g + 1) * SUBLANES])
            return jnp.concatenate(out, axis=0)

        out = dww_ref[FFN_KERNEL - 1:FFN_KERNEL, cols] * zc + dwb_ref[:, cols]
        for shift in range(1, FFN_KERNEL):
            out = out + dww_ref[FFN_KERNEL - 1 - shift:FFN_KERNEL - shift, cols] * shifted(shift)
        return out

    n_chunks = hidden // FFN_COL_CHUNK
    zs = []
    for j in range(n_chunks):
        zs.append(up(j))
        yield
    acc = None
    for j0 in range(0, n_chunks, FFN_DOWN_CHUNKS):
        acts = []
        for j in range(j0, min(j0 + FFN_DOWN_CHUNKS, n_chunks)):
            z_gate, z_val = [conv(c, zc) for c, zc in zs[j]]
            acts.append((_silu(z_gate) * z_val).astype(BF16))
        lo, hi = j0 * FFN_COL_CHUNK, (j0 + len(acts)) * FFN_COL_CHUNK
        part = _dot(jnp.concatenate(acts, axis=1), wdown_ref[lo:hi, :])
        acc = part if acc is None else acc + part
        if hi < hidden:
            yield
    for kk in range(d // LANES):
        for g, i in pieces:
            r0 = (g * ROW_STRIDE + i) * SUBLANES
            unperm_ref[kk, token_rows(g, i), :] = acc[r0:r0 + SUBLANES, kk * LANES:(kk + 1) * LANES]
    acc = jnp.concatenate([unperm_ref[kk] for kk in range(d // LANES)], axis=1)
    x2 = x1_ref[...] + gt2 * acc
    o_ref[...] = _rms(x2, fg_ref[...]) if final_norm else x2


N_MIXER_CONSTS = 15
N_FFN_CONSTS = 5


def _layer_kernel(x_ref, modm_ref, modf_ref, *refs, ts, n_seq, dqk, dgla, dconv, hidden, final_norm):
    mixer_consts = refs[:N_MIXER_CONSTS]
    ffn_consts = refs[N_MIXER_CONSTS:N_MIXER_CONSTS + N_FFN_CONSTS]
    (o_ref, st_ref, ubuf_ref, rbuf_ref, cbuf_ref, carry_ref, x1_ref, h2_ref,
     unperm_ref) = refs[N_MIXER_CONSTS + N_FFN_CONSTS:]
    i = pl.program_id(0)

    @pl.when(i % n_seq == 0)
    def _():
        st_ref[...] = jnp.zeros_like(st_ref)
        ubuf_ref[0:CONF_HALO, :] = jnp.zeros((CONF_HALO, dconv), F32)

    @pl.when((i + n_seq - 1) % n_seq == 0)
    def _():
        carry_ref[...] = jnp.zeros_like(carry_ref)

    @pl.when(i == 0)
    def _():
        x1_ref[...] = jnp.zeros_like(x1_ref)
        h2_ref[...] = jnp.zeros_like(h2_ref)

    slot = i % 2
    mixer = _mixer_stream(x_ref, modm_ref, *mixer_consts, x1_ref.at[slot], h2_ref.at[slot], st_ref,
                          ubuf_ref, rbuf_ref, cbuf_ref, ts=ts, dqk=dqk, dgla=dgla, dconv=dconv)
    ffn = _ffn_stream(x1_ref.at[1 - slot], h2_ref.at[1 - slot], modf_ref, *ffn_consts, o_ref, carry_ref,
                      unperm_ref, ts=ts, hidden=hidden, final_norm=final_norm)
    streams = {"F": ffn, "M": mixer}
    for who in STAGE_ORDER.replace(" ", ""):
        next(streams[who], None)
    for g in streams.values():
        for _ in g:
            pass


def _const_spec(shape):
    return pl.BlockSpec(shape, lambda i: (0,) * len(shape))


def _layer_call(x, mod, mixer_consts, ffn_consts, final_norm):
    bsz, seq, d = x.shape
    dqk = mixer_consts[4].shape[1]
    dconv = mixer_consts[11].shape[0]
    dgla = (mixer_consts[1].shape[1] - 2 * dqk) // 2
    hidden = ffn_consts[3].shape[0]
    ts = min(SEQ_TILE, seq)
    n_seq = seq // ts
    n_tiles = bsz * n_seq
    assert len(mixer_consts) == N_MIXER_CONSTS and len(ffn_consts) == N_FFN_CONSTS

    def mixer_tile(i):
        t = jnp.minimum(i, n_tiles - 1)
        return t // n_seq, t % n_seq

    def ffn_tile(i):
        t = jnp.maximum(i - 1, 0)
        return t // n_seq, t % n_seq

    kern = functools.partial(_layer_kernel, ts=ts, n_seq=n_seq, dqk=dqk, dgla=dgla, dconv=dconv,
                             hidden=hidden, final_norm=final_norm)
    return pl.pallas_call(
        kern,
        grid=(n_tiles + 1,),
        in_specs=[pl.BlockSpec((None, ts, d), lambda i: (*mixer_tile(i), 0)),
                  pl.BlockSpec((None, N_MOD, d), lambda i: (mixer_tile(i)[0], 0, 0)),
                  pl.BlockSpec((None, N_MOD, d), lambda i: (ffn_tile(i)[0], 0, 0))]
                 + [_const_spec(a.shape) for a in (*mixer_consts, *ffn_consts)],
        out_specs=pl.BlockSpec((None, ts, d), lambda i: (*ffn_tile(i), 0)),
        out_shape=jax.ShapeDtypeStruct(x.shape, F32),
        scratch_shapes=[pltpu.VMEM((dgla // GLA_HEADS, dqk), F32),
                        pltpu.VMEM((CONF_HALO + ts, dconv), F32),
                        pltpu.VMEM((SUBLANES - 1, CONF_HALO + ts, dconv), F32),
                        pltpu.VMEM((ts, dconv), F32),
                        pltpu.VMEM((2 * SUBLANES, 2 * hidden), F32),
                        pltpu.VMEM((2, ts, d), F32),
                        pltpu.VMEM((2, d // LANES, ts, LANES), F32),
                        pltpu.VMEM((d // LANES, ts, LANES), F32)],
        compiler_params=pltpu.CompilerParams(dimension_semantics=("arbitrary",),
                                             vmem_limit_bytes=VMEM_LIMIT_BYTES),
        name="layer",
    )(x, mod, mod, *mixer_consts, *ffn_consts)


def kernel(x, c, ada_w, ada_b, norm1_g, w_in, w_alpha, b_alpha, gla_norm_g, conf_dw_w, conf_dw_b,
           conf_gn_g, conf_gn_b, conf_pw_w, conf_pw_b, w_out, norm2_g, w_up, ffn_dw_w, ffn_dw_b,
           w_down, final_g):
    bsz, seq, d = x.shape
    depth = ada_w.shape[0]
    dconv = conf_pw_w.shape[1]
    n_main = w_in.shape[2] - GLA_LOWRANK - 2 * dconv
    assert seq % CHUNK == 0 and seq % min(SEQ_TILE, seq) == 0 and min(SEQ_TILE, seq) % ROW_GROUP == 0
    row = lambda a: a.reshape(1, -1)
    for l in range(depth):
        mod = _ada_call(c, ada_w[l], ada_b[l]).reshape(bsz, N_MOD, d)
        mixer_consts = (
            row(norm1_g[l]),
            w_in[l][:, :n_main].astype(BF16),
            w_in[l][:, n_main:n_main + GLA_LOWRANK].astype(BF16),
            w_in[l][:, n_main + GLA_LOWRANK:].astype(BF16),
            w_alpha[l].astype(BF16), row(b_alpha[l]), row(gla_norm_g[l]),
            conf_dw_w[l], row(conf_dw_b[l]), row(conf_gn_g[l]), row(conf_gn_b[l]),
            conf_pw_w[l].astype(BF16), row(conf_pw_b[l]), w_out[l].astype(BF16), row(norm2_g[l]))
        ffn_consts = (w_up[l].astype(BF16), ffn_dw_w[l], row(ffn_dw_b[l]),
                      w_down[l].astype(BF16), row(final_g))
        x = _layer_call(x, mod, mixer_consts, ffn_consts, final_norm=(l == depth - 1))
    return x
```
